```python
import jax, jax.numpy as jnp
from jax import lax
import numpy as np

D_MODEL = 1024
BATCH = 8
SEQ = 2048
DEPTH = 1
DEC_BATCH = 128
DEC_SEQ = 4
PAST_LEN = 16384
PAGE_SIZE = 128

N_META = 16
POOL_WIDTH = D_MODEL
POOL_WINDOWS = (2, 4, 8, 16)
N_POOL_GROUPS = 4
POOL_GROUP = POOL_WIDTH // N_POOL_GROUPS
POOL_BUF = max(POOL_WINDOWS) - 1
MLSTM_WIDTH = 2 * D_MODEL
N_HEADS = 4
HEAD_DIM = MLSTM_WIDTH // N_HEADS
CHUNK = 64
EPS = 1e-6
N_IN = 2 * POOL_WIDTH + 5 * MLSTM_WIDTH + 2 * N_HEADS + 2 * D_MODEL

kernel_name = 'hybrid_pool_mlstm_gated_decode_step'


def rmsnorm(x, g):
    xf = x.astype(jnp.float32)
    y = xf * lax.rsqrt(jnp.mean(xf * xf, axis=-1, keepdims=True) + EPS)
    return (y * g.astype(jnp.float32)).astype(x.dtype)


def split_projection(proj):
    sizes = (POOL_WIDTH, POOL_WIDTH, MLSTM_WIDTH, MLSTM_WIDTH, MLSTM_WIDTH, MLSTM_WIDTH, MLSTM_WIDTH,
             N_HEADS, N_HEADS, D_MODEL, D_MODEL)
    idx = np.cumsum(sizes)[:-1].tolist()
    return jnp.split(proj, idx, axis=-1)


def multiscale_pool(buf, a, pos0, w_pool, pool_scale):
    B, T, _ = a.shape
    ext = jnp.concatenate([buf.astype(jnp.float32), a.astype(jnp.float32)], axis=1)
    cs = jnp.cumsum(jnp.concatenate([jnp.zeros_like(ext[:, :1]), ext], axis=1), axis=1)
    pos = pos0 + jnp.arange(T)
    tok = ext[:, POOL_BUF:]
    outs = []
    for g, w in enumerate(POOL_WINDOWS):
        sl = slice(g * POOL_GROUP, (g + 1) * POOL_GROUP)
        hi = cs[:, POOL_BUF + 1:POOL_BUF + 1 + T, sl]
        lo = cs[:, POOL_BUF + 1 - w:POOL_BUF + 1 - w + T, sl]
        cnt = jnp.minimum(pos + 1, w).astype(jnp.float32)[None, :, None]
        outs.append((hi - lo) / cnt - tok[:, :, sl])
    pooled = jnp.stack(outs, axis=2).astype(a.dtype)
    mixed = jnp.einsum('btgc,gcd->btgd', pooled, w_pool).reshape(B, T, POOL_WIDTH)
    return mixed * pool_scale


def mlstm_chunk(carry, inp):
    C, n, m = carry
    q, k, v, ig, lf = inp
    L = q.shape[2]
    b = jnp.cumsum(lf, axis=-1)
    causal = jnp.tril(jnp.ones((L, L), dtype=bool))
    logw = jnp.where(causal, b[..., :, None] - b[..., None, :] + ig[..., None, :], -jnp.inf)
    log_inter = b + m[..., None]
    m_t = jnp.maximum(log_inter, jnp.max(logw, axis=-1))
    w_intra = jnp.exp(logw - m_t[..., None])
    w_inter = jnp.exp(log_inter - m_t)
    s = jnp.einsum('bhtd,bhsd->bhts', q, k) * w_intra
    num = jnp.einsum('bhts,bhsv->bhtv', s, v) + w_inter[..., None] * jnp.einsum('bhtd,bhdv->bhtv', q, C)
    den = jnp.sum(s, axis=-1) + w_inter * jnp.einsum('bhtd,bhd->bht', q, n)
    h = num / jnp.maximum(jnp.abs(den), jnp.exp(-m_t))[..., None]
    m_new = m_t[..., -1]
    w_state = jnp.exp(b[..., -1:] - b + ig - m_new[..., None])
    decay = jnp.exp(b[..., -1] + m - m_new)
    kw = k * w_state[..., None]
    C_new = decay[..., None, None] * C + jnp.einsum('bhsd,bhsv->bhdv', kw, v)
    n_new = decay[..., None] * n + jnp.sum(kw, axis=2)
    return (C_new, n_new, m_new), h


def mlstm_scan(q, k, v, ig, lf, state, n_lead):
    B, H, T, Dh = q.shape
    h_lead = []
    if n_lead > 0:
        state, h0 = mlstm_chunk(state, (q[:, :, :n_lead], k[:, :, :n_lead], v[:, :, :n_lead],
                                        ig[:, :, :n_lead], lf[:, :, :n_lead]))
        h_lead = [h0]
    rest = T - n_lead
    chunk = CHUNK if rest % CHUNK == 0 else rest
    nc = rest // chunk

    def blocks(t):
        t = t[:, :, n_lead:]
        t = t.reshape(t.shape[:2] + (nc, chunk) + t.shape[3:])
        return jnp.moveaxis(t, 2, 0)

    xs = (blocks(q), blocks(k), blocks(v), blocks(ig), blocks(lf))
    state, hs = lax.scan(mlstm_chunk, state, xs)
    hs = jnp.moveaxis(hs, 0, 2).reshape(B, H, rest, Dh)
    return state, jnp.concatenate(h_lead + [hs], axis=2)


def hybrid_layer(x, pos0, n_lead, pool_buf, C, n, m, norm_g, w_in, gate_bias, w_pool, pool_scale,
                 mh_norm_g, w_branch_pool, w_branch_mlstm, w_out):
    B, T, _ = x.shape
    f32 = jnp.float32
    hn = rmsnorm(x, norm_g)
    proj = jnp.einsum('btd,dn->btn', hn, w_in)
    a_pool, z_pool, q, k, v, o, z_m, i_pre, f_pre, g_pool, g_mlstm = split_projection(proj)

    y_pool = multiscale_pool(pool_buf, a_pool, pos0, w_pool, pool_scale) * jax.nn.silu(z_pool)
    new_buf = jnp.concatenate([pool_buf.astype(a_pool.dtype), a_pool], axis=1)[:, -POOL_BUF:]

    def heads(t):
        return t.reshape(B, T, N_HEADS, HEAD_DIM).transpose(0, 2, 1, 3).astype(f32)
    qh = heads(q)
    kh = heads(k) * (HEAD_DIM ** -0.5)
    vh = heads(v)
    gb = gate_bias.astype(f32)
    ig = (i_pre.astype(f32) + gb[:N_HEADS]).transpose(0, 2, 1)
    lf = jax.nn.log_sigmoid(f_pre.astype(f32) + gb[N_HEADS:]).transpose(0, 2, 1)
    state = (C.astype(f32), n.astype(f32), m.astype(f32))
    (C_new, n_new, m_new), h = mlstm_scan(qh, kh, vh, ig, lf, state, n_lead)
    h = h.transpose(0, 2, 1, 3)
    mu = jnp.mean(h, axis=-1, keepdims=True)
    var = jnp.mean(jnp.square(h - mu), axis=-1, keepdims=True)
    h = ((h - mu) * lax.rsqrt(var + EPS)).reshape(B, T, MLSTM_WIDTH) * mh_norm_g.astype(f32)
    y_mlstm = jax.nn.sigmoid(o) * h.astype(x.dtype) * jax.nn.silu(z_m)

    u = (jax.nn.sigmoid(g_pool) * jnp.einsum('btp,pd->btd', y_pool, w_branch_pool)
         + jax.nn.sigmoid(g_mlstm) * jnp.einsum('btm,md->btd', y_mlstm, w_branch_mlstm))
    x_out = x + jnp.einsum('btd,de->bte', u, w_out)
    return x_out, (new_buf, C_new, n_new, m_new)


def setup_inputs(seed: int = 0) -> dict:
    key = jax.random.key(seed)
    ks = jax.random.split(key, 20)
    f32 = jnp.float32

    def nrm(k, shape, s):
        return jax.random.normal(k, shape, f32) * s

    f_bias = jnp.linspace(3.0, 6.0, N_HEADS, dtype=f32)
    gate_bias = jnp.concatenate([nrm(ks[9], (DEPTH, N_HEADS), 0.1),
                                 f_bias[None] + nrm(ks[10], (DEPTH, N_HEADS), 0.1)], axis=-1)
    return {
        'x_prompt': nrm(ks[0], (BATCH, SEQ, D_MODEL), 1.0),
        'x_sample': nrm(ks[1], (DEC_BATCH, DEC_SEQ, D_MODEL), 1.0),
        'state_pool': nrm(ks[2], (DEPTH, DEC_BATCH, POOL_BUF, POOL_WIDTH), 1.0),
        'state_C': nrm(ks[3], (DEPTH, DEC_BATCH, N_HEADS, HEAD_DIM, HEAD_DIM), HEAD_DIM ** -0.5),
        'state_n': nrm(ks[4], (DEPTH, DEC_BATCH, N_HEADS, HEAD_DIM), 1.0),
        'state_m': nrm(ks[5], (DEPTH, DEC_BATCH, N_HEADS), 1.0),
        'meta_tokens': nrm(ks[6], (N_META, D_MODEL), 1.0),
        'norm_g': 1.0 + nrm(ks[7], (DEPTH, D_MODEL), 0.02),
        'w_in': nrm(ks[8], (DEPTH, D_MODEL, N_IN), D_MODEL ** -0.5),
        'gate_bias': gate_bias,
        'w_pool': nrm(ks[11], (DEPTH, N_POOL_GROUPS, POOL_GROUP, POOL_GROUP), POOL_GROUP ** -0.5),
        'pool_scale': 1.0 + nrm(ks[12], (DEPTH, POOL_WIDTH), 0.02),
        'mh_norm_g': 1.0 + nrm(ks[13], (DEPTH, MLSTM_WIDTH), 0.02),
        'w_branch_pool': nrm(ks[14], (DEPTH, POOL_WIDTH, D_MODEL), POOL_WIDTH ** -0.5),
        'w_branch_mlstm': nrm(ks[15], (DEPTH, MLSTM_WIDTH, D_MODEL), MLSTM_WIDTH ** -0.5),
        'w_out': nrm(ks[16], (DEPTH, D_MODEL, D_MODEL), D_MODEL ** -0.5),
        'final_g': 1.0 + nrm(ks[17], (D_MODEL,), 0.02),
    }


def reference(x_prompt, x_sample, state_pool, state_C, state_n, state_m, meta_tokens, norm_g, w_in,
              gate_bias, w_pool, pool_scale, mh_norm_g, w_branch_pool, w_branch_mlstm, w_out, final_g):
    bp = x_prompt.shape[0]
    meta = jnp.broadcast_to(meta_tokens[None].astype(x_prompt.dtype), (bp, N_META, D_MODEL))
    xp = jnp.concatenate([meta, x_prompt], axis=1)
    xs = x_sample
    new_p = ([], [], [], [])
    new_s = ([], [], [], [])
    for l in range(DEPTH):
        weights = (norm_g[l], w_in[l], gate_bias[l], w_pool[l], pool_scale[l], mh_norm_g[l],
                   w_branch_pool[l], w_branch_mlstm[l], w_out[l])
        zero_buf = jnp.zeros((bp, POOL_BUF, POOL_WIDTH), xp.dtype)
        zero_C = jnp.zeros((bp, N_HEADS, HEAD_DIM, HEAD_DIM), jnp.float32)
        zero_n = jnp.zeros((bp, N_HEADS, HEAD_DIM), jnp.float32)
        zero_m = jnp.zeros((bp, N_HEADS), jnp.float32)
        xp, sp = hybrid_layer(xp, 0, N_META, zero_buf, zero_C, zero_n, zero_m, *weights)
        xs, ss = hybrid_layer(xs, PAST_LEN, 0, state_pool[l], state_C[l], state_n[l], state_m[l], *weights)
        for lst, a in zip(new_p, sp):
            lst.append(a)
        for lst, a in zip(new_s, ss):
            lst.append(a)
    y_prompt = rmsnorm(xp, final_g)[:, N_META:]
    y_sample = rmsnorm(xs, final_g)
    pool_p = jnp.stack(new_p[0]).astype(x_prompt.dtype)
    C_p = jnp.stack(new_p[1]).astype(x_prompt.dtype)
    n_p = jnp.stack(new_p[2]).astype(x_prompt.dtype)
    m_p = jnp.stack(new_p[3]).astype(x_prompt.dtype)
    pool_s = jnp.stack(new_s[0]).astype(state_pool.dtype)
    C_s = jnp.stack(new_s[1]).astype(state_C.dtype)
    n_s = jnp.stack(new_s[2]).astype(state_n.dtype)
    m_s = jnp.stack(new_s[3]).astype(state_m.dtype)
    return (y_prompt, y_sample, pool_p, C_p, n_p, m_p, pool_s, C_s, n_s, m_s)
```

```python
import functools

import jax
import jax.numpy as jnp
from jax import lax
from jax.experimental import pallas as pl
from jax.experimental.pallas import tpu as pltpu

F32 = jnp.float32
BF16 = jnp.bfloat16

EPS = 1e-6
POOL_WINDOWS = (2, 4, 8, 16)
PAST_LEN = 16384
HALO = 16
GATE_LANES = 128
NEG_BIG = -1e30
VMEM_LIMIT = 56 * 1024 * 1024


def _sigmoid(x):
    return 1.0 / (1.0 + jnp.exp(-x))


def _silu(x):
    return x * _sigmoid(x)


def _log_sigmoid(x):
    return jnp.minimum(x, 0.0) - jnp.log1p(jnp.exp(-jnp.abs(x)))


def _proj_kernel(x_ref, g_ref, w_ref, wg_ref, main_ref, gates_ref, h_ref):
    @pl.when(pl.program_id(1) == 0)
    def _():
        x = x_ref[...]
        ms = jnp.mean(x * x, axis=-1, keepdims=True)
        hn = (x * lax.rsqrt(ms + EPS) * g_ref[...]).astype(BF16)
        h_ref[...] = hn
        gates_ref[...] = jnp.dot(hn, wg_ref[...], preferred_element_type=F32)

    main_ref[...] = jnp.dot(h_ref[...], w_ref[...], preferred_element_type=F32).astype(BF16)


def _projection(x2d, norm_g, w_main, w_gates, tm, tn):
    rows, d = x2d.shape
    n_main = w_main.shape[1]
    assert rows % tm == 0 and n_main % tn == 0
    return pl.pallas_call(
        _proj_kernel,
        grid=(rows // tm, n_main // tn),
        in_specs=[
            pl.BlockSpec((tm, d), lambda i, j: (i, 0)),
            pl.BlockSpec((1, d), lambda i, j: (0, 0)),
            pl.BlockSpec((d, tn), lambda i, j: (0, j)),
            pl.BlockSpec((d, GATE_LANES), lambda i, j: (0, 0)),
        ],
        out_specs=[
            pl.BlockSpec((tm, tn), lambda i, j: (i, j)),
            pl.BlockSpec((tm, GATE_LANES), lambda i, j: (i, 0)),
        ],
        out_shape=[
            jax.ShapeDtypeStruct((rows, n_main), BF16),
            jax.ShapeDtypeStruct((rows, GATE_LANES), F32),
        ],
        scratch_shapes=[pltpu.VMEM((tm, d), BF16)],
        compiler_params=pltpu.CompilerParams(
            dimension_semantics=("parallel", "arbitrary"),
            vmem_limit_bytes=VMEM_LIMIT),
        name="projection",
    )(x2d, norm_g, w_main, w_gates)


def _mlstm_kernel(q_ref, k_ref, v_ref, o_ref, zm_ref, gates_ref, bias_ref, mhg_ref,
                  c0_ref, nm0_ref, y_ref, c_ref, nm_ref, *pad_refs,
                  n_heads, chunk, t_blk, k_scale):
    h_idx = pl.program_id(1)
    L = chunk

    @pl.when(pl.program_id(2) == 0)
    def _():
        c_ref[...] = c0_ref[...]
        nm_ref[...] = nm0_ref[...]

    if t_blk == L:
        q = q_ref[0]
        k = k_ref[0]
        v = v_ref[0]
        g = gates_ref[0]
    else:
        pad_ref, gpad_ref = pad_refs

        def padded(ref):
            pad_ref[...] = jnp.zeros(pad_ref.shape, F32)
            pad_ref[0:t_blk, :] = ref[0].astype(F32)
            return pad_ref[...].astype(BF16)

        q = padded(q_ref)
        k = padded(k_ref)
        v = padded(v_ref)
        gpad_ref[...] = jnp.zeros(gpad_ref.shape, F32)
        gpad_ref[0:t_blk, :] = gates_ref[0]
        g = gpad_ref[...]

    lane = lax.broadcasted_iota(jnp.int32, (L, GATE_LANES), 1)
    row = lax.broadcasted_iota(jnp.int32, (L, GATE_LANES), 0)
    g = g + bias_ref[...]
    lf_all = _log_sigmoid(g)
    if t_blk != L:
        lf_all = jnp.where(row < t_blk, lf_all, 0.0)
        g = jnp.where(row < t_blk, g, NEG_BIG)

    tt = lax.broadcasted_iota(jnp.int32, (L, L), 0)
    ss = lax.broadcasted_iota(jnp.int32, (L, L), 1)
    causal = ss <= tt
    tril = jnp.where(causal, 1.0, 0.0).astype(F32)
    b_all = jnp.dot(tril, lf_all, precision=lax.Precision.HIGHEST, preferred_element_type=F32)

    def pick_col(mat, col):
        return jnp.sum(jnp.where(lane == col, mat, 0.0), axis=1, keepdims=True)

    ig_c = pick_col(g, h_idx)
    b_c = pick_col(b_all, n_heads + h_idx)

    z = jnp.where(lane < n_heads, g, b_all)
    sel_r = lax.broadcasted_iota(jnp.int32, (8, GATE_LANES), 0)
    sel_c = lax.broadcasted_iota(jnp.int32, (8, GATE_LANES), 1)
    head_rows = jnp.where(
        ((sel_r == 0) & (sel_c == h_idx)) | ((sel_r == 1) & (sel_c == n_heads + h_idx)), 1.0, 0.0
    ).astype(F32)
    zt = lax.dot_general(head_rows, z, (((1,), (1,)), ((), ())),
                         precision=lax.Precision.HIGHEST, preferred_element_type=F32)
    ig_r = zt[0:1, :]
    b_r = zt[1:2, :]

    m_prev = nm_ref[0, 0, 1:2, 0:1]
    n_prev = nm_ref[0, 0, 0:1, :]
    c_prev = c_ref[0, 0]

    log_inter = b_c + m_prev
    logw = jnp.where(causal, b_c - b_r + ig_r, NEG_BIG)
    m_t = jnp.maximum(log_inter, jnp.max(logw, axis=1, keepdims=True))
    w_intra = jnp.exp(logw - m_t)
    w_inter = jnp.exp(log_inter - m_t)

    qk = lax.dot_general(q, k, (((1,), (1,)), ((), ())), preferred_element_type=F32)
    s = qk * (w_intra * k_scale)
    num = (jnp.dot(s.astype(BF16), v, preferred_element_type=F32)
           + w_inter * jnp.dot(q, c_prev.astype(BF16), preferred_element_type=F32))
    qn = jnp.sum(q.astype(F32) * n_prev, axis=1, keepdims=True)
    den = jnp.sum(s, axis=1, keepdims=True) + w_inter * qn
    hh = num * (1.0 / jnp.maximum(jnp.abs(den), jnp.exp(-m_t)))

    mu = jnp.mean(hh, axis=1, keepdims=True)
    hc = hh - mu
    var = jnp.mean(hc * hc, axis=1, keepdims=True)
    hn = hc * lax.rsqrt(var + EPS) * mhg_ref[...]
    if t_blk == L:
        o = o_ref[0].astype(F32)
        zm = zm_ref[0].astype(F32)
        y_ref[0] = (_sigmoid(o) * hn * _silu(zm)).astype(BF16)
    else:
        o = o_ref[0].astype(F32)
        zm = zm_ref[0].astype(F32)
        y_ref[0] = (_sigmoid(o) * hn[0:t_blk, :] * _silu(zm)).astype(BF16)

    m_new = m_t[L - 1:L, :]
    b_last = b_c[L - 1:L, :]
    w_state = jnp.exp(b_last - b_c + ig_c - m_new)
    decay = jnp.exp(b_last + m_prev - m_new)
    kw = k.astype(F32) * (w_state * k_scale)
    c_ref[0, 0] = decay * c_prev + lax.dot_general(
        kw.astype(BF16), v, (((0,), (0,)), ((), ())), preferred_element_type=F32)
    nm_ref[0, 0, 0:1, :] = decay * n_prev + jnp.sum(kw, axis=0, keepdims=True)
    nm_ref[0, 0, 1:2, :] = jnp.broadcast_to(m_new, (1, nm_ref.shape[3]))


def _mlstm(main3, gates3, bias_row, mh_g, c0, nm0, *, n_heads, head_dim, chunk, col_q):
    B, T, _ = main3.shape
    H, Dh = n_heads, head_dim
    if T % chunk == 0:
        t_blk, L = chunk, chunk
    else:
        t_blk, L = T, HALO
        assert T <= L
    nc = T // t_blk
    b_init = c0.shape[0]
    cb = col_q // Dh
    group = (H * Dh) // Dh

    def col_spec(gi):
        return pl.BlockSpec((1, t_blk, Dh), lambda b, h, c: (b, c, cb + gi * group + h))

    init_b = (lambda b: b) if b_init == B else (lambda b: 0)
    kern = functools.partial(_mlstm_kernel, n_heads=H, chunk=L, t_blk=t_blk, k_scale=float(Dh) ** -0.5)
    scratch = []
    if t_blk != L:
        scratch = [pltpu.VMEM((L, Dh), F32), pltpu.VMEM((L, GATE_LANES), F32)]
    return pl.pallas_call(
        kern,
        grid=(B, H, nc),
        in_specs=[
            col_spec(0), col_spec(1), col_spec(2), col_spec(3), col_spec(4),
            pl.BlockSpec((1, t_blk, GATE_LANES), lambda b, h, c: (b, c, 0)),
            pl.BlockSpec((1, GATE_LANES), lambda b, h, c: (0, 0)),
            pl.BlockSpec((1, Dh), lambda b, h, c: (0, h)),
            pl.BlockSpec((1, 1, Dh, Dh), lambda b, h, c: (init_b(b), h, 0, 0)),
            pl.BlockSpec((1, 1, 8, Dh), lambda b, h, c: (init_b(b), h, 0, 0)),
        ],
        out_specs=[
            pl.BlockSpec((1, t_blk, Dh), lambda b, h, c: (b, c, h)),
            pl.BlockSpec((1, 1, Dh, Dh), lambda b, h, c: (b, h, 0, 0)),
            pl.BlockSpec((1, 1, 8, Dh), lambda b, h, c: (b, h, 0, 0)),
        ],
        out_shape=[
            jax.ShapeDtypeStruct((B, T, H * Dh), BF16),
            jax.ShapeDtypeStruct((B, H, Dh, Dh), F32),
            jax.ShapeDtypeStruct((B, H, 8, Dh), F32),
        ],
        scratch_shapes=scratch,
        compiler_params=pltpu.CompilerParams(
            dimension_semantics=("parallel", "parallel", "arbitrary"),
            vmem_limit_bytes=VMEM_LIMIT),
        name="mlstm",
    )(main3, main3, main3, main3, main3, gates3, bias_row, mh_g, c0, nm0)


def _merge_kernel(a_ref, halo_ref, buf0_ref, zp_ref, ym_ref, gp_ref, gm_ref, x_ref,
                  wpool_ref, pscale_ref, wbp_ref, wbm_ref, wout_ref, fg_ref,
                  y_ref, ext_ref, *, tm, pos0, group_width, single_tile):
    i = pl.program_id(1)
    if single_tile:
        ext_ref[0:HALO, :] = buf0_ref[0]
    else:
        ext_ref[0:HALO, :] = jnp.where(i == 0, buf0_ref[0], halo_ref[0].astype(F32))
    ext_ref[HALO:HALO + tm, :] = a_ref[0].astype(F32)

    tok_row = lax.broadcasted_iota(jnp.int32, (tm, 1), 0) + i * tm + pos0
    mixed = []
    for gi, w in enumerate(POOL_WINDOWS):
        c0 = gi * group_width
        cols = slice(c0, c0 + group_width)
        tok = ext_ref[HALO:HALO + tm, cols]
        acc = tok
        for d in range(1, w):
            acc = acc + ext_ref[HALO - d:HALO - d + tm, cols]
        if pos0 + 1 >= w:
            mean = acc * (1.0 / w)
        else:
            cnt = jnp.minimum(tok_row + 1, w).astype(F32)
            mean = acc * (1.0 / cnt)
        pooled = (mean - tok).astype(BF16)
        mixed.append(jnp.dot(pooled, wpool_ref[gi], preferred_element_type=F32))
    mixed = jnp.concatenate(mixed, axis=1)
    y_pool = mixed * pscale_ref[...] * _silu(zp_ref[0].astype(F32))

    u_pool = jnp.dot(y_pool.astype(BF16), wbp_ref[...], preferred_element_type=F32)
    u_m = jnp.dot(ym_ref[0], wbm_ref[...], preferred_element_type=F32)
    u = _sigmoid(gp_ref[0].astype(F32)) * u_pool + _sigmoid(gm_ref[0].astype(F32)) * u_m
    xo = x_ref[0] + jnp.dot(u.astype(BF16), wout_ref[...], preferred_element_type=F32)
    ms = jnp.mean(xo * xo, axis=-1, keepdims=True)
    y_ref[0] = xo * lax.rsqrt(ms + EPS) * fg_ref[...]


def _merge(main3, ymlstm3, x3, buf0, w_pool, pool_scale, w_bp, w_bm, w_out, final_g,
           *, tm, pos0, col_gates):
    B, T, _ = main3.shape
    D = x3.shape[2]
    P = pool_scale.shape[1]
    M = ymlstm3.shape[2]
    G, gw, _ = w_pool.shape
    assert T % tm == 0 and (tm % HALO == 0 or T == tm)
    nt = T // tm
    hb = tm // HALO if tm % HALO == 0 else 1
    cgp = col_gates // D
    kern = functools.partial(_merge_kernel, tm=tm, pos0=pos0, group_width=gw, single_tile=nt == 1)
    const2 = lambda b, i: (0, 0)
    return pl.pallas_call(
        kern,
        grid=(B, nt),
        in_specs=[
            pl.BlockSpec((1, tm, P), lambda b, i: (b, i, 0)),
            pl.BlockSpec((1, min(HALO, T), P),
                         lambda b, i: (b, jnp.maximum(i * hb - 1, 0), 0)),
            pl.BlockSpec((1, HALO, P), lambda b, i: (b if buf0.shape[0] == B else 0, 0, 0)),
            pl.BlockSpec((1, tm, P), lambda b, i: (b, i, 1)),
            pl.BlockSpec((1, tm, M), lambda b, i: (b, i, 0)),
            pl.BlockSpec((1, tm, D), lambda b, i: (b, i, cgp)),
            pl.BlockSpec((1, tm, D), lambda b, i: (b, i, cgp + 1)),
            pl.BlockSpec((1, tm, D), lambda b, i: (b, i, 0)),
            pl.BlockSpec((G, gw, gw), lambda b, i: (0, 0, 0)),
            pl.BlockSpec((1, P), const2),
            pl.BlockSpec((P, D), const2),
            pl.BlockSpec((M, D), const2),
            pl.BlockSpec((D, D), const2),
            pl.BlockSpec((1, D), const2),
        ],
        out_specs=pl.BlockSpec((1, tm, D), lambda b, i: (b, i, 0)),
        out_shape=jax.ShapeDtypeStruct((B, T, D), F32),
        scratch_shapes=[pltpu.VMEM((HALO + tm, P), F32)],
        compiler_params=pltpu.CompilerParams(
            dimension_semantics=("parallel", "arbitrary"),
            vmem_limit_bytes=VMEM_LIMIT),
        name="merge",
    )(main3, main3, buf0, main3, ymlstm3, main3, main3, x3,
      w_pool, pool_scale, w_bp, w_bm, w_out, final_g)


def _pick_tile(n, candidates):
    for c in candidates:
        if n % c == 0:
            return c
    return n


def kernel(x_prompt, x_sample, state_pool, state_C, state_n, state_m, meta_tokens, norm_g, w_in,
           gate_bias, w_pool, pool_scale, mh_norm_g, w_branch_pool, w_branch_mlstm, w_out, final_g):
    assert norm_g.shape[0] == 1, "one layer only"
    Bp, S, D = x_prompt.shape
    Bs, Ts, _ = x_sample.shape
    n_meta = meta_tokens.shape[0]
    P = pool_scale.shape[1]
    M = mh_norm_g.shape[1]
    H = gate_bias.shape[1] // 2
    Dh = M // H
    pool_buf = state_pool.shape[2]
    assert P == D and M % D == 0 and n_meta == HALO and pool_buf == HALO - 1

    n_front = 2 * P + 5 * M
    w = w_in[0]
    w_main = jnp.concatenate([w[:, :n_front], w[:, n_front + 2 * H:]], axis=1).astype(BF16)
    w_gates = jnp.pad(w[:, n_front:n_front + 2 * H], ((0, 0), (0, GATE_LANES - 2 * H))).astype(BF16)
    bias_row = jnp.pad(gate_bias[0], (0, GATE_LANES - 2 * H)).reshape(1, GATE_LANES).astype(F32)
    ng = norm_g[0].reshape(1, D)
    mh_g = mh_norm_g[0].reshape(1, M)
    wpool = w_pool[0].astype(BF16)
    pscale = pool_scale[0].reshape(1, P)
    w_bp = w_branch_pool[0].astype(BF16)
    w_bm = w_branch_mlstm[0].astype(BF16)
    wo = w_out[0].astype(BF16)
    fg = final_g.reshape(1, D)
    n_main = w_main.shape[1]
    tn = _pick_tile(n_main, (512, 256, 128))

    def project(x2d):
        rows = x2d.shape[0]
        tm = _pick_tile(rows, (1024, 512, 256, 128, 64, 32, 16, 8))
        return _projection(x2d, ng, w_main, w_gates, tm, tn)

    def pack_nm(n, m):
        B = n.shape[0]
        rows = jnp.concatenate(
            [n[:, :, None, :], jnp.broadcast_to(m[:, :, None, None], (B, H, 1, Dh)),
             jnp.zeros((B, H, 6, Dh), F32)], axis=2)
        return rows

    mlstm = functools.partial(_mlstm, n_heads=H, head_dim=Dh, col_q=2 * P)

    main_m, gates_m = project(meta_tokens.astype(F32))
    _, c_meta, nm_meta = mlstm(
        main_m[None], gates_m[None], bias_row, mh_g,
        jnp.zeros((1, H, Dh, Dh), F32), jnp.zeros((1, H, 8, Dh), F32), chunk=n_meta)
    buf_meta = main_m[None, :, :P].astype(F32)

    main_p, gates_p = project(x_prompt.reshape(Bp * S, D))
    main_p = main_p.reshape(Bp, S, n_main)
    y_m_p, c_p, nm_p = mlstm(main_p, gates_p.reshape(Bp, S, GATE_LANES), bias_row, mh_g,
                             c_meta, nm_meta, chunk=_pick_tile(S, (256, 128, 64, 32, 16)))
    y_prompt = _merge(main_p, y_m_p, x_prompt, buf_meta, wpool, pscale, w_bp, w_bm, wo, fg,
                      tm=_pick_tile(S, (256, 128, 64, 32, 16)), pos0=n_meta, col_gates=n_front)
    pool_p = jnp.concatenate(
        [jnp.broadcast_to(buf_meta[:, 1:], (Bp, pool_buf, P)), main_p[:, :, :P].astype(F32)],
        axis=1)[:, -pool_buf:]

    main_s, gates_s = project(x_sample.reshape(Bs * Ts, D))
    main_s = main_s.reshape(Bs, Ts, n_main)
    y_m_s, c_s, nm_s = mlstm(main_s, gates_s.reshape(Bs, Ts, GATE_LANES), bias_row, mh_g,
                             state_C[0].astype(F32), pack_nm(state_n[0].astype(F32), state_m[0].astype(F32)),
                             chunk=256)
    buf_s = jnp.pad(state_pool[0].astype(F32), ((0, 0), (HALO - pool_buf, 0), (0, 0)))
    y_sample = _merge(main_s, y_m_s, x_sample, buf_s, wpool, pscale, w_bp, w_bm, wo, fg,
                      tm=Ts, pos0=PAST_LEN, col_gates=n_front)
    pool_s = jnp.concatenate([state_pool[0].astype(F32), main_s[:, :, :P].astype(F32)], axis=1)[:, -pool_buf:]

    dt = x_prompt.dtype
    return (y_prompt, y_sample,
            pool_p[None].astype(dt), c_p[None].astype(dt),
            nm_p[None, :, :, 0, :].astype(dt), nm_p[None, :, :, 1, 0].astype(dt),
            pool_s[None].astype(state_pool.dtype), c_s[None].astype(state_C.dtype),
            nm_s[None, :, :, 0, :].astype(state_n.dtype), nm_s[None, :, :, 1, 0].astype(state_m.dtype))
```
